```python
import jax, jax.numpy as jnp
from jax import lax
import numpy as np

D_MODEL = 1024
BATCH = 32
SEQ = 2048
DEPTH = 2
DEC_BATCH = 4
DEC_SEQ = 8192
PAST_LEN = 128

GRID_W = 64
WIN_R = 8
WIN_C = 16
NA_HEADS = 8
NA_HEAD_DIM = 64
NA_WIDTH = NA_HEADS * NA_HEAD_DIM
FT_GROUPS = 4
FT_GROUP_DIM = 64
FT_WIDTH = FT_GROUPS * FT_GROUP_DIM
CV_WIDTH = 256
CV_KERNEL = 31
N_BRANCH = 3
NORM_EPS = 1e-6
NEG_INF = -1e30
SPLIT_SIZES = (NA_WIDTH, NA_WIDTH, NA_WIDTH, NA_WIDTH,
               FT_WIDTH, FT_WIDTH,
               CV_WIDTH, CV_WIDTH, CV_WIDTH,
               D_MODEL, D_MODEL, D_MODEL)
D_IN = sum(SPLIT_SIZES)

kernel_name = "hybrid_natten_fnet_conformer_encoder"


def rms_norm(x, g):
    x32 = x.astype(jnp.float32)
    y = x32 * lax.rsqrt(jnp.mean(x32 * x32, axis=-1, keepdims=True) + NORM_EPS)
    return (y * g.astype(jnp.float32)).astype(x.dtype)


def layer_norm(x, g, b):
    x32 = x.astype(jnp.float32)
    mu = jnp.mean(x32, axis=-1, keepdims=True)
    xc = x32 - mu
    var = jnp.mean(xc * xc, axis=-1, keepdims=True)
    y = xc * lax.rsqrt(var + NORM_EPS) * g.astype(jnp.float32) + b.astype(jnp.float32)
    return y.astype(x.dtype)


def neighbourhood_attention(q, k, v, rel_pos_bias):
    b, l = q.shape[0], q.shape[1]
    rows = l // GRID_W
    wr = min(WIN_R, rows)
    qg = q.reshape(b, rows, GRID_W, NA_HEADS, NA_HEAD_DIM) * (NA_HEAD_DIM ** -0.5)
    kg = k.reshape(b, rows, GRID_W, NA_HEADS, NA_HEAD_DIM)
    vg = v.reshape(b, rows, GRID_W, NA_HEADS, NA_HEAD_DIM)
    col = jnp.arange(GRID_W)
    c_start = jnp.clip(col - WIN_C // 2, 0, GRID_W - WIN_C)
    col_valid = (col[None, :] >= c_start[:, None]) & (col[None, :] < c_start[:, None] + WIN_C)
    dc_idx = jnp.clip(col[None, :] - col[:, None] + WIN_C - 1, 0, 2 * WIN_C - 2)
    bias_c = rel_pos_bias[:, :, dc_idx]

    def row_block(r):
        r_start = jnp.clip(r - wr // 2, 0, rows - wr)
        k_blk = lax.dynamic_slice_in_dim(kg, r_start, wr, axis=1)
        v_blk = lax.dynamic_slice_in_dim(vg, r_start, wr, axis=1)
        q_row = lax.dynamic_index_in_dim(qg, r, axis=1, keepdims=False)
        dr_idx = r_start + jnp.arange(wr) - r + WIN_R - 1
        bias = jnp.take(bias_c, dr_idx, axis=1).transpose(0, 2, 1, 3)
        s = jnp.einsum('bqhd,bikhd->bhqik', q_row, k_blk).astype(jnp.float32)
        s = s + bias[None].astype(jnp.float32)
        s = jnp.where(col_valid[None, None, :, None, :], s, NEG_INF)
        p = jax.nn.softmax(s.reshape(b, NA_HEADS, GRID_W, wr * GRID_W), axis=-1)
        p = p.reshape(b, NA_HEADS, GRID_W, wr, GRID_W).astype(v.dtype)
        return jnp.einsum('bhqik,bikhd->bqhd', p, v_blk)

    out = lax.map(row_block, jnp.arange(rows))
    return out.transpose(1, 0, 2, 3, 4).reshape(b, l, NA_WIDTH)


def fourier_mix(u):
    b, l, _ = u.shape
    ug = u.reshape(b, l, FT_GROUPS, FT_GROUP_DIM).astype(jnp.float32)
    f = jnp.fft.fft2(ug, axes=(1, 3), norm="ortho").real
    return f.reshape(b, l, FT_WIDTH).astype(u.dtype)


def conformer_conv(u_val, u_glu_gate, dw_w, dw_b, ln_g, ln_b):
    h = u_val * jax.nn.sigmoid(u_glu_gate)
    h = lax.conv_general_dilated(
        h, dw_w.astype(h.dtype)[:, None, :], window_strides=(1,),
        padding=[(CV_KERNEL // 2, CV_KERNEL // 2)],
        dimension_numbers=('NWC', 'WIO', 'NWC'), feature_group_count=CV_WIDTH) + dw_b
    return jax.nn.silu(layer_norm(h, ln_g, ln_b))


def encoder_layer(x, pre_g, post_g, w_in, rpb, dw_w, dw_b, cn_g, cn_b, w_a, w_b, w_c, w_o):
    b, l, _ = x.shape
    h = rms_norm(x, pre_g)
    z = h @ w_in
    split_pts = np.cumsum(SPLIT_SIZES)[:-1].tolist()
    (q, k, v, gate_a, u_b, gate_b, c_val, c_glu, gate_c,
     merge_a, merge_b, merge_c) = jnp.split(z, split_pts, axis=-1)
    heads = lambda t: t.reshape(b, l, NA_HEADS, NA_HEAD_DIM)
    y_a = neighbourhood_attention(heads(q), heads(k), heads(v), rpb) * jax.nn.silu(gate_a)
    y_b = fourier_mix(u_b) * jax.nn.silu(gate_b)
    y_c = conformer_conv(c_val, c_glu, dw_w, dw_b, cn_g, cn_b) * jax.nn.silu(gate_c)
    merged = (jax.nn.sigmoid(merge_a) * (y_a @ w_a)
              + jax.nn.sigmoid(merge_b) * (y_b @ w_b)
              + jax.nn.sigmoid(merge_c) * (y_c @ w_c))
    return x + rms_norm(merged @ w_o, post_g)


def setup_inputs(seed: int = 0) -> dict:
    key = jax.random.key(seed)
    ks = jax.random.split(key, 16)
    f32 = jnp.float32
    nrm = lambda k, shape, scale: (jax.random.normal(k, shape, f32) * scale).astype(f32)
    return {
        "x_prompt": nrm(ks[0], (BATCH, SEQ, D_MODEL), 1.0),
        "x_sample": nrm(ks[1], (DEC_BATCH, DEC_SEQ, D_MODEL), 1.0),
        "pre_norm_g": 1.0 + nrm(ks[2], (DEPTH, D_MODEL), 0.05),
        "post_norm_g": 1.0 + nrm(ks[3], (DEPTH, D_MODEL), 0.05),
        "w_in": nrm(ks[4], (DEPTH, D_MODEL, D_IN), D_MODEL ** -0.5),
        "rel_pos_bias": nrm(ks[5], (DEPTH, NA_HEADS, 2 * WIN_R - 1, 2 * WIN_C - 1), 0.5),
        "c_dw_w": nrm(ks[6], (DEPTH, CV_KERNEL, CV_WIDTH), CV_KERNEL ** -0.5),
        "c_dw_b": nrm(ks[7], (DEPTH, CV_WIDTH), 0.02),
        "c_norm_g": 1.0 + nrm(ks[8], (DEPTH, CV_WIDTH), 0.05),
        "c_norm_b": nrm(ks[9], (DEPTH, CV_WIDTH), 0.02),
        "w_a_out": nrm(ks[10], (DEPTH, NA_WIDTH, D_MODEL), NA_WIDTH ** -0.5),
        "w_b_out": nrm(ks[11], (DEPTH, FT_WIDTH, D_MODEL), FT_WIDTH ** -0.5),
        "w_c_out": nrm(ks[12], (DEPTH, CV_WIDTH, D_MODEL), CV_WIDTH ** -0.5),
        "w_o": nrm(ks[13], (DEPTH, D_MODEL, D_MODEL), D_MODEL ** -0.5),
    }


def reference(x_prompt, x_sample, pre_norm_g, post_norm_g, w_in, rel_pos_bias, c_dw_w, c_dw_b,
              c_norm_g, c_norm_b, w_a_out, w_b_out, w_c_out, w_o):
    y_prompt = x_prompt
    y_sample = x_sample
    for i in range(DEPTH):
        params = (pre_norm_g[i], post_norm_g[i], w_in[i], rel_pos_bias[i], c_dw_w[i], c_dw_b[i],
                  c_norm_g[i], c_norm_b[i], w_a_out[i], w_b_out[i], w_c_out[i], w_o[i])
        y_prompt = encoder_layer(y_prompt, *params)
        y_sample = encoder_layer(y_sample, *params)
    return (y_prompt, y_sample)
```

```python
import functools
import math

import jax
import jax.numpy as jnp
from jax import lax
from jax.experimental import pallas as pl
from jax.experimental.pallas import tpu as pltpu

D_MODEL = 1024
GRID_W = 64
WIN_R = 8
WIN_C = 16
NA_HEADS = 8
NA_HEAD_DIM = 64
NA_WIDTH = NA_HEADS * NA_HEAD_DIM
FT_GROUPS = 4
FT_GROUP_DIM = 64
FT_WIDTH = FT_GROUPS * FT_GROUP_DIM
CV_WIDTH = 256
CV_KERNEL = 31
NORM_EPS = 1e-6
NEG_INF = -1e30

_Q0, _K0, _V0, _GA0, _UB0, _GB0, _CV0, _CG0, _GC0, _MA0, _MB0, _MC0, _DIN = (
    0, 512, 1024, 1536, 2048, 2304, 2560, 2816, 3072, 3328, 4352, 5376, 6400)

LANES = 128
HEADS_PER_STEP = LANES // NA_HEAD_DIM
VMEM_LIMIT_BYTES = 56 * 1024 * 1024

F32 = jnp.float32
BF16 = jnp.bfloat16


def _params(sem):
    return pltpu.CompilerParams(dimension_semantics=sem, vmem_limit_bytes=VMEM_LIMIT_BYTES)


def _const_spec(shape):
    n = len(shape)
    return pl.BlockSpec(shape, lambda *_: (0,) * n)


def _rms_norm(x, g):
    ms = jnp.mean(x * x, axis=-1, keepdims=True)
    return x * lax.rsqrt(ms + NORM_EPS) * g


def _split_scratch(rows, width):
    return pltpu.VMEM((width // LANES, rows, LANES), F32)


def _store_split(ref, row, size, val):
    for g in range(ref.shape[0]):
        ref[g, pl.ds(row, size), :] = val[:, g * LANES:(g + 1) * LANES]


def _load_split(ref, start, size, stride):
    return jnp.concatenate([ref[g, pl.ds(start, size, stride=stride), :]
                            for g in range(ref.shape[0])], axis=-1)


def _fft_factors(seq):
    log = int(math.log2(seq))
    assert 1 << log == seq
    l1 = 1 << (log // 2)
    return l1, seq // l1


def _proj_kernel(x_ref, g_ref, w_ref, q_ref, k_ref, v_ref, u_ref, c_ref, us_ref, *, l1):
    x = x_ref[0]
    h = _rms_norm(x, g_ref[...]).astype(BF16)

    def proj(lo, hi):
        return jnp.dot(h, w_ref[:, lo:hi], preferred_element_type=F32)

    q_ref[0] = (proj(0, 512) * (NA_HEAD_DIM ** -0.5)).astype(BF16)
    k_ref[0] = proj(512, 1024).astype(BF16)
    v_ref[0] = proj(1024, 1536).astype(BF16)
    c_val = proj(1792, 2048)
    c_glu = proj(2048, 2304)
    c_ref[0] = (c_val * jax.nn.sigmoid(c_glu)).astype(BF16)
    _store_split(us_ref, 0, us_ref.shape[1], proj(1536, 1792))
    rows = us_ref.shape[1] // l1
    for i in range(l1):
        u_ref[0, i] = _load_split(us_ref, i, rows, l1).astype(BF16)


def _proj_call(x, pre_g, w1, *, tile):
    b, seq, _ = x.shape
    l1, l2 = _fft_factors(seq)
    assert seq % tile == 0 and tile % (16 * l1) == 0
    grid = (b, seq // tile)
    tok = lambda width: pl.BlockSpec((1, tile, width), lambda i, j: (i, j, 0))
    out_shape = (
        jax.ShapeDtypeStruct((b, seq, NA_WIDTH), BF16),
        jax.ShapeDtypeStruct((b, seq, NA_WIDTH), BF16),
        jax.ShapeDtypeStruct((b, seq, NA_WIDTH), BF16),
        jax.ShapeDtypeStruct((b, l1, l2, FT_WIDTH), BF16),
        jax.ShapeDtypeStruct((b, seq, CV_WIDTH), BF16),
    )
    return pl.pallas_call(
        functools.partial(_proj_kernel, l1=l1),
        grid=grid,
        in_specs=[tok(D_MODEL), _const_spec((1, D_MODEL)), _const_spec(w1.shape)],
        out_specs=(tok(NA_WIDTH), tok(NA_WIDTH), tok(NA_WIDTH),
                   pl.BlockSpec((1, l1, tile // l1, FT_WIDTH), lambda i, j: (i, 0, j, 0)),
                   tok(CV_WIDTH)),
        out_shape=out_shape,
        scratch_shapes=[_split_scratch(tile, FT_WIDTH)],
        compiler_params=_params(("parallel", "parallel")),
        name="proj",
    )(x, pre_g, w1)


def _fnet_kernel(u_ref, m_ref, g_ref, cb_ref, sb_ref, o_ref, ar_ref, ai_ref, w_ref, *, l1, l2, scale):
    def stage1(a, carry):
        res = jnp.dot(m_ref[a], u_ref[0, a], preferred_element_type=F32)
        row = pl.multiple_of(a * l2, l2)
        _store_split(ar_ref, row, l2, res[:l2])
        _store_split(ai_ref, row, l2, res[l2:])
        return carry

    lax.fori_loop(0, l1, stage1, 0)

    def stage2(k2, carry):
        r = jnp.concatenate([_load_split(ar_ref, k2, l1, l2),
                             _load_split(ai_ref, k2, l1, l2)], axis=0).astype(BF16)
        w = jnp.dot(g_ref[...], r, preferred_element_type=F32)
        y = (jnp.dot(w[:l1].astype(BF16), cb_ref[...], preferred_element_type=F32)
             + jnp.dot(w[l1:].astype(BF16), sb_ref[...], preferred_element_type=F32))
        _store_split(w_ref, pl.multiple_of(k2 * l1, l1), l1, y * scale)
        return carry

    lax.fori_loop(0, l2, stage2, 0)

    def stage3(k1, carry):
        o_ref[0, pl.ds(pl.multiple_of(k1 * l2, l2), l2), :] = (
            _load_split(w_ref, k1, l2, l1).astype(BF16))
        return carry

    lax.fori_loop(0, l1, stage3, 0)


def _fnet_tables(seq):
    l1, l2 = _fft_factors(seq)
    a = jnp.arange(l1, dtype=jnp.int32)[:, None, None]
    k2 = jnp.arange(l2, dtype=jnp.int32)[None, :, None]
    c = jnp.arange(l2, dtype=jnp.int32)[None, None, :]
    ang = (2.0 * math.pi / seq) * ((k2 * (a + l1 * c)) % seq).astype(F32)
    m = jnp.concatenate([jnp.cos(ang), -jnp.sin(ang)], axis=1).astype(BF16)
    k1 = jnp.arange(l1, dtype=jnp.int32)
    ang1 = (2.0 * math.pi / l1) * ((k1[:, None] * k1[None, :]) % l1).astype(F32)
    c1, s1 = jnp.cos(ang1), jnp.sin(ang1)
    g = jnp.concatenate([jnp.concatenate([c1, s1], axis=1),
                         jnp.concatenate([-s1, c1], axis=1)], axis=0).astype(BF16)
    ch = jnp.arange(FT_WIDTH, dtype=jnp.int32)
    same = (ch[:, None] // FT_GROUP_DIM) == (ch[None, :] // FT_GROUP_DIM)
    angc = (2.0 * math.pi / FT_GROUP_DIM) * (
        ((ch[:, None] % FT_GROUP_DIM) * (ch[None, :] % FT_GROUP_DIM)) % FT_GROUP_DIM).astype(F32)
    cb = jnp.where(same, jnp.cos(angc), 0.0).astype(BF16)
    sb = jnp.where(same, jnp.sin(angc), 0.0).astype(BF16)
    return m, g, cb, sb


def _fnet_call(u, tables):
    b, l1, l2, width = u.shape
    seq = l1 * l2
    m, g, cb, sb = tables
    scale = 1.0 / math.sqrt(seq * FT_GROUP_DIM)
    return pl.pallas_call(
        functools.partial(_fnet_kernel, l1=l1, l2=l2, scale=scale),
        grid=(b,),
        in_specs=[pl.BlockSpec((1, l1, l2, width), lambda i: (i, 0, 0, 0)),
                  _const_spec(m.shape), _const_spec(g.shape),
                  _const_spec(cb.shape), _const_spec(sb.shape)],
        out_specs=pl.BlockSpec((1, seq, width), lambda i: (i, 0, 0)),
        out_shape=jax.ShapeDtypeStruct((b, seq, width), BF16),
        scratch_shapes=[_split_scratch(seq, width)] * 3,
        compiler_params=_params(("parallel",)),
        name="fnet",
    )(u, m, g, cb, sb)


CONV_ROWS = 32
CONV_HALO = 16
assert CV_KERNEL // 2 < CONV_HALO


def _conv_kernel(c_ref, w_ref, b_ref, o_ref, pad_ref, *, seq):
    zeros = jnp.zeros((CONV_HALO, CV_WIDTH), F32)
    pad_ref[0:CONV_HALO, :] = zeros
    pad_ref[seq + CONV_HALO:seq + 2 * CONV_HALO, :] = zeros
    chunk = 256

    def fill(i, carry):
        row = pl.multiple_of(i * chunk, chunk)
        pad_ref[pl.ds(row + CONV_HALO, chunk), :] = c_ref[0, pl.ds(row, chunk), :].astype(F32)
        return carry

    lax.fori_loop(0, seq // chunk, fill, 0)

    w = w_ref[...]
    bias = b_ref[...]
    span = CONV_ROWS + 2 * CONV_HALO
    shifted = span - 8

    def step(i, carry):
        row = pl.multiple_of(i * CONV_ROWS, CONV_ROWS)
        win = pad_ref[pl.ds(row, span), :]
        acc = jnp.zeros((CONV_ROWS, CV_WIDTH), F32) + bias
        for r in range(8):
            sh = win[r:r + shifted]
            for mlt in range(shifted // 8):
                off = 8 * mlt + r
                j = off - (CONV_HALO - CV_KERNEL // 2)
                if 0 <= j < CV_KERNEL and 8 * mlt + CONV_ROWS <= shifted:
                    acc = acc + w[j:j + 1, :] * sh[8 * mlt:8 * mlt + CONV_ROWS]
        o_ref[0, pl.ds(row, CONV_ROWS), :] = acc
        return carry

    lax.fori_loop(0, seq // CONV_ROWS, step, 0)


def _conv_call(c, dw_w, dw_b):
    b, seq, width = c.shape
    return pl.pallas_call(
        functools.partial(_conv_kernel, seq=seq),
        grid=(b,),
        in_specs=[pl.BlockSpec((1, seq, width), lambda i: (i, 0, 0)),
                  _const_spec(dw_w.shape), _const_spec(dw_b.shape)],
        out_specs=pl.BlockSpec((1, seq, width), lambda i: (i, 0, 0)),
        out_shape=jax.ShapeDtypeStruct((b, seq, width), F32),
        scratch_shapes=[pltpu.VMEM((seq + 2 * CONV_HALO, width), F32)],
        compiler_params=_params(("parallel",)),
        name="conv",
    )(c, dw_w, dw_b)


def _natt_kernel(q_ref, k_ref, v_ref, t_ref, o_ref, *, rows):
    lane = lax.broadcasted_iota(jnp.int32, (GRID_W, LANES), 1)
    first = lane < NA_HEAD_DIM
    win = WIN_R * GRID_W

    def body(r, carry):
        r_start = jnp.clip(r - WIN_R // 2, 0, rows - WIN_R)
        qrow = pl.multiple_of(r * GRID_W, GRID_W)
        krow = pl.multiple_of(r_start * GRID_W, GRID_W)
        q = q_ref[0, pl.ds(qrow, GRID_W), :].astype(F32)
        q2 = jnp.concatenate([jnp.where(first, q, 0.0), jnp.where(first, 0.0, q)], axis=0).astype(BF16)
        kw = k_ref[0, pl.ds(krow, win), :]
        vw = v_ref[0, pl.ds(krow, win), :]
        s = lax.dot_general(q2, kw, (((1,), (1,)), ((), ())), preferred_element_type=F32)
        s = s + t_ref[0, r - r_start]
        m = jnp.max(s, axis=-1, keepdims=True)
        p = jnp.exp(s - m)
        denom = jnp.sum(p, axis=-1, keepdims=True)
        o = jnp.dot(p.astype(BF16), vw, preferred_element_type=F32) / denom
        o_ref[0, pl.ds(qrow, GRID_W), :] = jnp.where(first, o[:GRID_W], o[GRID_W:]).astype(BF16)
        return carry

    lax.fori_loop(0, rows, body, 0)


def _natt_table(rpb):
    col = jnp.arange(GRID_W)
    c_start = jnp.clip(col - WIN_C // 2, 0, GRID_W - WIN_C)
    valid = (col[None, :] >= c_start[:, None]) & (col[None, :] < c_start[:, None] + WIN_C)
    dc = jnp.clip(col[None, :] - col[:, None] + WIN_C - 1, 0, 2 * WIN_C - 2)
    off = jnp.arange(WIN_R)
    dr = off[None, :] - off[:, None] + WIN_R - 1
    t = rpb[:, dr][:, :, :, dc]
    t = jnp.where(valid[None, None, None], t, NEG_INF)
    t = t.transpose(0, 1, 3, 2, 4).reshape(NA_HEADS // HEADS_PER_STEP, HEADS_PER_STEP, WIN_R,
                                           GRID_W, WIN_R * GRID_W)
    return t.transpose(0, 2, 1, 3, 4).reshape(NA_HEADS // HEADS_PER_STEP, WIN_R,
                                              HEADS_PER_STEP * GRID_W, WIN_R * GRID_W).astype(F32)


def _natt_call(q, k, v, table):
    b, seq, _ = q.shape
    rows = seq // GRID_W
    assert rows >= WIN_R
    blk = pl.BlockSpec((1, seq, LANES), lambda i, j: (i, 0, j))
    return pl.pallas_call(
        functools.partial(_natt_kernel, rows=rows),
        grid=(b, NA_HEADS // HEADS_PER_STEP),
        in_specs=[blk, blk, blk,
                  pl.BlockSpec((1,) + table.shape[1:], lambda i, j: (j, 0, 0, 0))],
        out_specs=blk,
        out_shape=jax.ShapeDtypeStruct((b, seq, NA_WIDTH), BF16),
        compiler_params=_params(("parallel", "parallel")),
        name="natt",
    )(q, k, v, table)


def _out_kernel(x_ref, ya_ref, yb_ref, yc_ref, pre_ref, post_ref, cg_ref, cb_ref,
                w2_ref, wa_ref, wb_ref, wc_ref, wo_ref, o_ref):
    x = x_ref[0]
    h = _rms_norm(x, pre_ref[...]).astype(BF16)

    def proj(lo, hi):
        return jnp.dot(h, w2_ref[:, lo:hi], preferred_element_type=F32)

    def branch(y, gate_lo, gate_hi, w_ref, merge_lo):
        gated = (y * jax.nn.silu(proj(gate_lo, gate_hi))).astype(BF16)
        p = jnp.dot(gated, w_ref[...], preferred_element_type=F32)
        return jax.nn.sigmoid(proj(merge_lo, merge_lo + D_MODEL)) * p

    conv = yc_ref[0]
    mu = jnp.mean(conv, axis=-1, keepdims=True)
    cen = conv - mu
    var = jnp.mean(cen * cen, axis=-1, keepdims=True)
    y_c = jax.nn.silu(cen * lax.rsqrt(var + NORM_EPS) * cg_ref[...] + cb_ref[...])

    merged = branch(ya_ref[0].astype(F32), 0, 512, wa_ref, 1024)
    merged = merged + branch(yb_ref[0].astype(F32), 512, 768, wb_ref, 2048)
    merged = merged + branch(y_c, 768, 1024, wc_ref, 3072)
    o = jnp.dot(merged.astype(BF16), wo_ref[...], preferred_element_type=F32)
    o_ref[0] = x + _rms_norm(o, post_ref[...])


def _out_call(x, ya, yb, yc, pre_g, post_g, cn_g, cn_b, w2, wa, wb, wc, wo, *, tile):
    b, seq, _ = x.shape
    assert seq % tile == 0
    tok = lambda width: pl.BlockSpec((1, tile, width), lambda i, j: (i, j, 0))
    consts = (pre_g, post_g, cn_g, cn_b, w2, wa, wb, wc, wo)
    return pl.pallas_call(
        _out_kernel,
        grid=(b, seq // tile),
        in_specs=[tok(D_MODEL), tok(NA_WIDTH), tok(FT_WIDTH), tok(CV_WIDTH)]
                 + [_const_spec(c.shape) for c in consts],
        out_specs=tok(D_MODEL),
        out_shape=jax.ShapeDtypeStruct(x.shape, x.dtype),
        compiler_params=_params(("parallel", "parallel")),
        name="out",
    )(x, ya, yb, yc, *consts)


PROJ_TILE = 1024
OUT_TILE = 512


def _layer_weights(pre_g, post_g, w_in, rpb, dw_w, dw_b, cn_g, cn_b, w_a, w_b, w_c, w_o):
    cols = lambda *ranges: jnp.concatenate([w_in[:, lo:hi] for lo, hi in ranges], axis=1).astype(BF16)
    row = lambda vec: vec.reshape(1, -1).astype(F32)
    return dict(
        w1=cols((_Q0, _GA0), (_UB0, _GB0), (_CV0, _GC0)),
        w2=cols((_GA0, _UB0), (_GB0, _CV0), (_GC0, _DIN)),
        table=_natt_table(rpb),
        pre_g=row(pre_g), post_g=row(post_g), cn_g=row(cn_g), cn_b=row(cn_b),
        dw_w=dw_w.astype(F32), dw_b=row(dw_b),
        wa=w_a.astype(BF16), wb=w_b.astype(BF16), wc=w_c.astype(BF16), wo=w_o.astype(BF16),
    )


def _encoder_layer(x, p, tables, *, proj_tile=PROJ_TILE, out_tile=OUT_TILE):
    q, k, v, u, c = _proj_call(x, p["pre_g"], p["w1"], tile=proj_tile)
    yb = _fnet_call(u, tables)
    yc = _conv_call(c, p["dw_w"], p["dw_b"])
    ya = _natt_call(q, k, v, p["table"])
    return _out_call(x, ya, yb, yc, p["pre_g"], p["post_g"], p["cn_g"], p["cn_b"],
                     p["w2"], p["wa"], p["wb"], p["wc"], p["wo"], tile=out_tile)


def kernel(x_prompt, x_sample, pre_norm_g, post_norm_g, w_in, rel_pos_bias, c_dw_w, c_dw_b,
           c_norm_g, c_norm_b, w_a_out, w_b_out, w_c_out, w_o):
    depth = w_in.shape[0]
    tab_prompt = _fnet_tables(x_prompt.shape[1])
    tab_sample = _fnet_tables(x_sample.shape[1])
    y_prompt, y_sample = x_prompt, x_sample
    for i in range(depth):
        p = _layer_weights(pre_norm_g[i], post_norm_g[i], w_in[i], rel_pos_bias[i], c_dw_w[i],
                           c_dw_b[i], c_norm_g[i], c_norm_b[i], w_a_out[i], w_b_out[i],
                           w_c_out[i], w_o[i])
        y_prompt = _encoder_layer(y_prompt, p, tab_prompt)
        y_sample = _encoder_layer(y_sample, p, tab_sample)
    return (y_prompt, y_sample)
```

```python
import functools
import math

import jax
import jax.numpy as jnp
from jax import lax
from jax.experimental import pallas as pl
from jax.experimental.pallas import tpu as pltpu

D_MODEL = 1024
GRID_W = 64
WIN_R = 8
WIN_C = 16
NA_HEADS = 8
NA_HEAD_DIM = 64
NA_WIDTH = NA_HEADS * NA_HEAD_DIM
FT_GROUPS = 4
FT_GROUP_DIM = 64
FT_WIDTH = FT_GROUPS * FT_GROUP_DIM
CV_WIDTH = 256
CV_KERNEL = 31
NORM_EPS = 1e-6
NEG_INF = -1e30

_Q0, _K0, _V0, _GA0, _UB0, _GB0, _CV0, _CG0, _GC0, _MA0, _MB0, _MC0, _DIN = (
    0, 512, 1024, 1536, 2048, 2304, 2560, 2816, 3072, 3328, 4352, 5376, 6400)

LANES = 128
SUBLANES = 8
MXU_DIM = 256
HEADS_PER_STEP = LANES // NA_HEAD_DIM
VMEM_LIMIT_BYTES = 56 * 1024 * 1024

F32 = jnp.float32
BF16 = jnp.bfloat16


def _params(sem):
    return pltpu.CompilerParams(dimension_semantics=sem, vmem_limit_bytes=VMEM_LIMIT_BYTES)


def _const_spec(shape):
    n = len(shape)
    return pl.BlockSpec(shape, lambda *_: (0,) * n)


def _rms_norm(x, g):
    ms = jnp.mean(x * x, axis=-1, keepdims=True)
    return x * lax.rsqrt(ms + NORM_EPS) * g


def _pitch(block_rows):
    assert block_rows % (2 * SUBLANES) == 0
    return block_rows + SUBLANES


def _blocked_scratch(blocks, block_rows, width):
    return pltpu.VMEM((width // LANES, blocks * _pitch(block_rows), LANES), F32)


def _store_block(ref, block, block_rows, val, first_group=0):
    row = block * _pitch(block_rows)
    if not isinstance(row, int):
        row = pl.multiple_of(row, SUBLANES)
    for g in range(val.shape[-1] // LANES):
        ref[first_group + g, pl.ds(row, block_rows), :] = val[:, g * LANES:(g + 1) * LANES]


def _load_row_of_blocks(ref, row, blocks, block_rows, first_group=0, groups=None):
    groups = ref.shape[0] if groups is None else groups
    return jnp.concatenate([ref[first_group + g, pl.ds(row, blocks, stride=_pitch(block_rows)), :]
                            for g in range(groups)], axis=-1)


def _fft_factors(seq):
    log = int(math.log2(seq))
    assert 1 << log == seq
    l1 = 1 << (log // 2)
    return l1, seq // l1


def _proj_kernel(x_ref, g_ref, w_ref, cs_ref, q_ref, k_ref, v_ref, u_ref, c_ref, us_ref, *, l1):
    x = x_ref[0]
    h = _rms_norm(x, g_ref[...]).astype(BF16)

    def proj(lo, hi):
        return jnp.dot(h, w_ref[:, lo:hi], preferred_element_type=F32)

    q_ref[0] = (proj(0, 512) * (NA_HEAD_DIM ** -0.5)).astype(BF16)
    k_ref[0] = proj(512, 1024).astype(BF16)
    v_ref[0] = proj(1024, 1536).astype(BF16)
    c_val = proj(1792, 2048)
    c_glu = proj(2048, 2304)
    c_ref[0] = (c_val * jax.nn.sigmoid(c_glu)).astype(BF16)
    spec = jnp.dot(proj(1536, 1792).astype(BF16), cs_ref[...], preferred_element_type=F32)
    rows = x.shape[0] // l1
    for c in range(rows):
        _store_block(us_ref, c, l1, spec[c * l1:(c + 1) * l1])
    groups = FT_WIDTH // LANES
    for i in range(l1):
        for part in range(2):
            u_ref[0, i, part] = _load_row_of_blocks(
                us_ref, i, rows, l1, part * groups, groups).astype(BF16)


def _proj_call(x, pre_g, w1, cs, *, tile):
    b, seq, _ = x.shape
    l1, l2 = _fft_factors(seq)
    assert seq % tile == 0 and tile % (16 * l1) == 0
    grid = (b, seq // tile)
    tok = lambda width: pl.BlockSpec((1, tile, width), lambda i, j: (i, j, 0))
    out_shape = (
        jax.ShapeDtypeStruct((b, seq, NA_WIDTH), BF16),
        jax.ShapeDtypeStruct((b, seq, NA_WIDTH), BF16),
        jax.ShapeDtypeStruct((b, seq, NA_WIDTH), BF16),
        jax.ShapeDtypeStruct((b, l1, 2, l2, FT_WIDTH), BF16),
        jax.ShapeDtypeStruct((b, seq, CV_WIDTH), BF16),
    )
    return pl.pallas_call(
        functools.partial(_proj_kernel, l1=l1),
        grid=grid,
        in_specs=[tok(D_MODEL), _const_spec((1, D_MODEL)), _const_spec(w1.shape), _const_spec(cs.shape)],
        out_specs=(tok(NA_WIDTH), tok(NA_WIDTH), tok(NA_WIDTH),
                   pl.BlockSpec((1, l1, 2, tile // l1, FT_WIDTH), lambda i, j: (i, 0, 0, j, 0)),
                   tok(CV_WIDTH)),
        out_shape=out_shape,
        scratch_shapes=[_blocked_scratch(tile // l1, l1, 2 * FT_WIDTH)],
        compiler_params=_params(("parallel", "parallel")),
        name="proj",
    )(x, pre_g, w1, cs)


def _fnet_kernel(u_ref, f_ref, g_ref, o_ref, ar_ref, ai_ref, w_ref, *, l1, l2, n1, n2, scale):
    width = o_ref.shape[-1]

    def stage1(i, carry):
        a0 = i * n1
        x = u_ref[0, pl.ds(a0, n1)].reshape(n1 * 2 * l2, width)
        res = jnp.dot(f_ref[...], x, preferred_element_type=F32)
        for j in range(n1):
            _store_block(ar_ref, a0 + j, l2, res[2 * j * l2:(2 * j + 1) * l2])
            _store_block(ai_ref, a0 + j, l2, res[(2 * j + 1) * l2:(2 * j + 2) * l2])
        return carry

    lax.fori_loop(0, l1 // n1, stage1, 0, unroll=2)

    def stage2(i, carry):
        k0 = i * n2
        parts = []
        for j in range(n2):
            parts.append(_load_row_of_blocks(ar_ref, k0 + j, l1, l2))
            parts.append(_load_row_of_blocks(ai_ref, k0 + j, l1, l2))
        r = jnp.concatenate(parts, axis=0).astype(BF16)
        w = jnp.dot(g_ref[i], r, preferred_element_type=F32) * scale
        for j in range(n2):
            _store_block(w_ref, k0 + j, l1, w[j * l1:(j + 1) * l1])
        return carry

    lax.fori_loop(0, l2 // n2, stage2, 0, unroll=2)

    def stage3(k1, carry):
        o_ref[0, pl.ds(pl.multiple_of(k1 * l2, l2), l2), :] = (
            _load_row_of_blocks(w_ref, k1, l2, l1).astype(BF16))
        return carry

    lax.fori_loop(0, l1, stage3, 0, unroll=2)


def _fnet_blocking(l1, l2):
    return max(1, MXU_DIM // (2 * l2)), max(1, MXU_DIM // (2 * l1))


def _block_diag(blocks):
    n, r, c = blocks.shape
    eye = jnp.eye(n, dtype=blocks.dtype)
    return (eye[:, None, :, None] * blocks[:, :, None, :]).reshape(n * r, n * c)


def _fnet_tables(seq):
    l1, l2 = _fft_factors(seq)
    n1, n2 = _fnet_blocking(l1, l2)
    ch = jnp.arange(FT_WIDTH, dtype=jnp.int32)
    same = (ch[:, None] // FT_GROUP_DIM) == (ch[None, :] // FT_GROUP_DIM)
    angc = (2.0 * math.pi / FT_GROUP_DIM) * (
        ((ch[:, None] % FT_GROUP_DIM) * (ch[None, :] % FT_GROUP_DIM)) % FT_GROUP_DIM).astype(F32)
    cs = jnp.concatenate([jnp.where(same, jnp.cos(angc), 0.0),
                          jnp.where(same, -jnp.sin(angc), 0.0)], axis=1).astype(BF16)
    i2 = jnp.arange(l2, dtype=jnp.int32)
    ang2 = (2.0 * math.pi / l2) * ((i2[:, None] * i2[None, :]) % l2).astype(F32)
    c2, s2 = jnp.cos(ang2), jnp.sin(ang2)
    f2 = jnp.concatenate([jnp.concatenate([c2, s2], axis=1),
                          jnp.concatenate([-s2, c2], axis=1)], axis=0)
    f = _block_diag(jnp.broadcast_to(f2, (n1,) + f2.shape)).astype(BF16)
    k2 = jnp.arange(l2, dtype=jnp.int32)[:, None, None]
    k1 = jnp.arange(l1, dtype=jnp.int32)[None, :, None]
    a = jnp.arange(l1, dtype=jnp.int32)[None, None, :]
    ang1 = (2.0 * math.pi / seq) * ((a * (l2 * k1 + k2)) % seq).astype(F32)
    g2 = jnp.concatenate([jnp.cos(ang1), jnp.sin(ang1)], axis=2)
    g = jax.vmap(_block_diag)(g2.reshape(l2 // n2, n2, l1, 2 * l1)).astype(BF16)
    return cs, f, g


def _fnet_call(u, f, g, *, width):
    b, l1, _, l2, full = u.shape
    seq = l1 * l2
    n1, n2 = _fnet_blocking(l1, l2)
    scale = 1.0 / math.sqrt(seq * FT_GROUP_DIM)
    return pl.pallas_call(
        functools.partial(_fnet_kernel, l1=l1, l2=l2, n1=n1, n2=n2, scale=scale),
        grid=(b, full // width),
        in_specs=[pl.BlockSpec((1, l1, 2, l2, width), lambda i, j: (i, 0, 0, 0, j)),
                  _const_spec(f.shape), _const_spec(g.shape)],
        out_specs=pl.BlockSpec((1, seq, width), lambda i, j: (i, 0, j)),
        out_shape=jax.ShapeDtypeStruct((b, seq, full), BF16),
        scratch_shapes=[_blocked_scratch(l1, l2, width), _blocked_scratch(l1, l2, width),
                        _blocked_scratch(l2, l1, width)],
        compiler_params=_params(("parallel", "parallel")),
        name="fnet",
    )(u, f, g)


CONV_ROWS = 32
CONV_HALO = 16
assert CV_KERNEL // 2 < CONV_HALO


def _conv_kernel(c_ref, w_ref, b_ref, o_ref, pad_ref, *, seq):
    zeros = jnp.zeros((CONV_HALO, CV_WIDTH), F32)
    pad_ref[0:CONV_HALO, :] = zeros
    pad_ref[seq + CONV_HALO:seq + 2 * CONV_HALO, :] = zeros
    chunk = 256

    def fill(i, carry):
        row = pl.multiple_of(i * chunk, chunk)
        pad_ref[pl.ds(row + CONV_HALO, chunk), :] = c_ref[0, pl.ds(row, chunk), :].astype(F32)
        return carry

    lax.fori_loop(0, seq // chunk, fill, 0)

    span = CONV_ROWS + 2 * CONV_HALO
    reps = CONV_ROWS // SUBLANES

    def step(i, carry):
        row = pl.multiple_of(i * CONV_ROWS, CONV_ROWS)
        win = pad_ref[pl.ds(row, span), :]
        acc = jnp.zeros((CONV_ROWS, CV_WIDTH), F32) + b_ref[...]
        for r in range(SUBLANES):
            sh = win if r == 0 else pltpu.roll(win, span - r, axis=0)
            for mlt in range(span // SUBLANES):
                j = SUBLANES * mlt + r - (CONV_HALO - CV_KERNEL // 2)
                if 0 <= j < CV_KERNEL and SUBLANES * mlt + r + CONV_ROWS <= span:
                    wj = jnp.concatenate([w_ref[j]] * reps, axis=0)
                    acc = acc + wj * sh[SUBLANES * mlt:SUBLANES * mlt + CONV_ROWS]
        o_ref[0, pl.ds(row, CONV_ROWS), :] = acc
        return carry

    lax.fori_loop(0, seq // CONV_ROWS, step, 0)


def _conv_call(c, dw_w, dw_b):
    b, seq, width = c.shape
    return pl.pallas_call(
        functools.partial(_conv_kernel, seq=seq),
        grid=(b,),
        in_specs=[pl.BlockSpec((1, seq, width), lambda i: (i, 0, 0)),
                  _const_spec(dw_w.shape), _const_spec(dw_b.shape)],
        out_specs=pl.BlockSpec((1, seq, width), lambda i: (i, 0, 0)),
        out_shape=jax.ShapeDtypeStruct((b, seq, width), F32),
        scratch_shapes=[pltpu.VMEM((seq + 2 * CONV_HALO, width), F32)],
        compiler_params=_params(("parallel",)),
        name="conv",
    )(c, dw_w, dw_b)


NATT_BODY_ROWS = 2
NATT_STEP_ROWS = 128


def _natt_kernel(q_ref, k_ref, v_ref, t_ref, o_ref, s0_ref, s1_ref, p0_ref, p1_ref, d0_ref, d1_ref,
                 *, rows):
    seqs = q_ref.shape[0]
    n_rows = seqs * rows
    lane = lax.broadcasted_iota(jnp.int32, (GRID_W, LANES), 1)
    first = lane < NA_HEAD_DIM
    win = WIN_R * GRID_W
    s_banks, p_banks, d_banks = (s0_ref, s1_ref), (p0_ref, p1_ref), (d0_ref, d1_ref)
    row_bits = rows.bit_length() - 1
    assert 1 << row_bits == rows

    def locate(i):
        if isinstance(i, int):
            b, r = divmod(i, rows)
            r_start = min(max(r - WIN_R // 2, 0), rows - WIN_R)
            return b, r * GRID_W, r_start * GRID_W, r - r_start
        b = lax.shift_right_logical(i, row_bits)
        r = lax.bitwise_and(i, rows - 1)
        r_start = jnp.clip(r - WIN_R // 2, 0, rows - WIN_R)
        return (b, pl.multiple_of(r * GRID_W, GRID_W), pl.multiple_of(r_start * GRID_W, GRID_W),
                r - r_start)

    def scores(i, s_ref, slot):
        b, qrow, krow, edge = locate(i)
        q = q_ref[b, pl.ds(qrow, GRID_W), :].astype(F32)
        q2 = jnp.concatenate([jnp.where(first, q, 0.0), jnp.where(first, 0.0, q)], axis=0).astype(BF16)
        kw = k_ref[b, pl.ds(krow, win), :]
        s = lax.dot_general(q2, kw, (((1,), (1,)), ((), ())), preferred_element_type=F32)
        s_ref[slot] = s + t_ref[0, edge]

    def softmax(s_ref, p_ref, d_ref, slot):
        s = s_ref[slot]
        p = jnp.exp(s - jnp.max(s, axis=-1, keepdims=True))
        d_ref[slot] = jnp.broadcast_to(jnp.sum(p, axis=-1, keepdims=True), d_ref.shape[1:])
        p_ref[slot] = p.astype(BF16)

    def output(i, p_ref, d_ref, slot):
        b, qrow, krow, _ = locate(i)
        vw = v_ref[b, pl.ds(krow, win), :]
        o = jnp.dot(p_ref[slot], vw, preferred_element_type=F32) / d_ref[slot]
        o_ref[b, pl.ds(qrow, GRID_W), :] = jnp.where(first, o[:GRID_W], o[GRID_W:]).astype(BF16)

    def body(m, bank, *, do_scores=True, do_softmax=True, do_output=True):
        for slot in range(NATT_BODY_ROWS):
            if do_output:
                output((m - 2) * NATT_BODY_ROWS + slot, p_banks[bank], d_banks[bank], slot)
        for slot in range(NATT_BODY_ROWS):
            if do_softmax:
                softmax(s_banks[1 - bank], p_banks[1 - bank], d_banks[1 - bank], slot)
        for slot in range(NATT_BODY_ROWS):
            if do_scores:
                scores(m * NATT_BODY_ROWS + slot, s_banks[bank], slot)

    n_bodies = n_rows // NATT_BODY_ROWS
    assert n_bodies % 2 == 0 and n_bodies >= 2
    body(0, 0, do_softmax=False, do_output=False)
    body(1, 1, do_output=False)

    def steady(j, carry):
        body(2 * j, 0)
        body(2 * j + 1, 1)
        return carry

    lax.fori_loop(1, n_bodies // 2, steady, 0)
    body(n_bodies, 0, do_scores=False)
    body(n_bodies + 1, 1, do_scores=False, do_softmax=False)


def _natt_table(rpb):
    col = jnp.arange(GRID_W)
    c_start = jnp.clip(col - WIN_C // 2, 0, GRID_W - WIN_C)
    valid = (col[None, :] >= c_start[:, None]) & (col[None, :] < c_start[:, None] + WIN_C)
    dc = jnp.clip(col[None, :] - col[:, None] + WIN_C - 1, 0, 2 * WIN_C - 2)
    off = jnp.arange(WIN_R)
    dr = off[None, :] - off[:, None] + WIN_R - 1
    t = rpb[:, dr][:, :, :, dc]
    t = jnp.where(valid[None, None, None], t, NEG_INF)
    t = t.transpose(0, 1, 3, 2, 4).reshape(NA_HEADS // HEADS_PER_STEP, HEADS_PER_STEP, WIN_R,
                                           GRID_W, WIN_R * GRID_W)
    return t.transpose(0, 2, 1, 3, 4).reshape(NA_HEADS // HEADS_PER_STEP, WIN_R,
                                              HEADS_PER_STEP * GRID_W, WIN_R * GRID_W).astype(F32)


def _natt_call(q, k, v, table):
    b, seq, _ = q.shape
    rows = seq // GRID_W
    assert rows >= WIN_R
    seqs = max(1, min(b, NATT_STEP_ROWS // rows))
    assert b % seqs == 0
    blk = pl.BlockSpec((seqs, seq, LANES), lambda i, j: (i, 0, j))
    stacked = HEADS_PER_STEP * GRID_W
    scratch = ([pltpu.VMEM((NATT_BODY_ROWS, stacked, WIN_R * GRID_W), F32)] * 2
               + [pltpu.VMEM((NATT_BODY_ROWS, stacked, WIN_R * GRID_W), BF16)] * 2
               + [pltpu.VMEM((NATT_BODY_ROWS, stacked, LANES), F32)] * 2)
    return pl.pallas_call(
        functools.partial(_natt_kernel, rows=rows),
        grid=(b // seqs, NA_HEADS // HEADS_PER_STEP),
        in_specs=[blk, blk, blk,
                  pl.BlockSpec((1,) + table.shape[1:], lambda i, j: (j, 0, 0, 0))],
        out_specs=blk,
        out_shape=jax.ShapeDtypeStruct((b, seq, NA_WIDTH), BF16),
        scratch_shapes=scratch,
        compiler_params=_params(("parallel", "parallel")),
        name="natt",
    )(q, k, v, table)


def _out_kernel(x_ref, ya_ref, yb_ref, yc_ref, pre_ref, post_ref, cg_ref, cb_ref,
                w2_ref, wa_ref, wb_ref, wc_ref, wo_ref, o_ref):
    x = x_ref[0]
    h = _rms_norm(x, pre_ref[...]).astype(BF16)

    def proj(lo, hi):
        return jnp.dot(h, w2_ref[:, lo:hi], preferred_element_type=F32)

    def branch(y, gate_lo, gate_hi, w_ref, merge_lo):
        gated = (y * jax.nn.silu(proj(gate_lo, gate_hi))).astype(BF16)
        p = jnp.dot(gated, w_ref[...], preferred_element_type=F32)
        return jax.nn.sigmoid(proj(merge_lo, merge_lo + D_MODEL)) * p

    conv = yc_ref[0]
    mu = jnp.mean(conv, axis=-1, keepdims=True)
    cen = conv - mu
    var = jnp.mean(cen * cen, axis=-1, keepdims=True)
    y_c = jax.nn.silu(cen * lax.rsqrt(var + NORM_EPS) * cg_ref[...] + cb_ref[...])

    merged = branch(ya_ref[0].astype(F32), 0, 512, wa_ref, 1024)
    merged = merged + branch(yb_ref[0].astype(F32), 512, 768, wb_ref, 2048)
    merged = merged + branch(y_c, 768, 1024, wc_ref, 3072)
    o = jnp.dot(merged.astype(BF16), wo_ref[...], preferred_element_type=F32)
    o_ref[0] = x + _rms_norm(o, post_ref[...])


def _out_call(x, ya, yb, yc, pre_g, post_g, cn_g, cn_b, w2, wa, wb, wc, wo, *, tile):
    b, seq, _ = x.shape
    assert seq % tile == 0
    tok = lambda width: pl.BlockSpec((1, tile, width), lambda i, j: (i, j, 0))
    consts = (pre_g, post_g, cn_g, cn_b, w2, wa, wb, wc, wo)
    return pl.pallas_call(
        _out_kernel,
        grid=(b, seq // tile),
        in_specs=[tok(D_MODEL), tok(NA_WIDTH), tok(FT_WIDTH), tok(CV_WIDTH)]
                 + [_const_spec(c.shape) for c in consts],
        out_specs=tok(D_MODEL),
        out_shape=jax.ShapeDtypeStruct(x.shape, x.dtype),
        compiler_params=_params(("parallel", "parallel")),
        name="out",
    )(x, ya, yb, yc, *consts)


PROJ_TILE = 1024
OUT_TILE = 512
FNET_SCRATCH_BYTES = 16 * 1024 * 1024


def _fnet_width(seq):
    width = FT_WIDTH
    while 3 * seq * width * 4 > FNET_SCRATCH_BYTES and width > LANES:
        width //= 2
    return width


def _layer_weights(pre_g, post_g, w_in, rpb, dw_w, dw_b, cn_g, cn_b, w_a, w_b, w_c, w_o):
    cols = lambda *ranges: jnp.concatenate([w_in[:, lo:hi] for lo, hi in ranges], axis=1).astype(BF16)
    row = lambda vec: vec.reshape(1, -1).astype(F32)
    return dict(
        w1=cols((_Q0, _GA0), (_UB0, _GB0), (_CV0, _GC0)),
        w2=cols((_GA0, _UB0), (_GB0, _CV0), (_GC0, _DIN)),
        table=_natt_table(rpb),
        pre_g=row(pre_g), post_g=row(post_g), cn_g=row(cn_g), cn_b=row(cn_b),
        dw_w=jnp.broadcast_to(dw_w.astype(F32)[:, None, :], (CV_KERNEL, SUBLANES, CV_WIDTH)),
        dw_b=row(dw_b),
        wa=w_a.astype(BF16), wb=w_b.astype(BF16), wc=w_c.astype(BF16), wo=w_o.astype(BF16),
    )


def _encoder_layer(x, p, tables, *, proj_tile=PROJ_TILE, out_tile=OUT_TILE):
    cs, f, g = tables
    q, k, v, u, c = _proj_call(x, p["pre_g"], p["w1"], cs, tile=proj_tile)
    yb = _fnet_call(u, f, g, width=_fnet_width(x.shape[1]))
    yc = _conv_call(c, p["dw_w"], p["dw_b"])
    ya = _natt_call(q, k, v, p["table"])
    return _out_call(x, ya, yb, yc, p["pre_g"], p["post_g"], p["cn_g"], p["cn_b"],
                     p["w2"], p["wa"], p["wb"], p["wc"], p["wo"], tile=out_tile)


def kernel(x_prompt, x_sample, pre_norm_g, post_norm_g, w_in, rel_pos_bias, c_dw_w, c_dw_b,
           c_norm_g, c_norm_b, w_a_out, w_b_out, w_c_out, w_o):
    depth = w_in.shape[0]
    tab_prompt = _fnet_tables(x_prompt.shape[1])
    tab_sample = _fnet_tables(x_sample.shape[1])
    y_prompt, y_sample = x_prompt, x_sample
    for i in range(depth):
        p = _layer_weights(pre_norm_g[i], post_norm_g[i], w_in[i], rel_pos_bias[i], c_dw_w[i],
                           c_dw_b[i], c_norm_g[i], c_norm_b[i], w_a_out[i], w_b_out[i],
                           w_c_out[i], w_o[i])
        y_prompt = _encoder_layer(y_prompt, p, tab_prompt)
        y_sample = _encoder_layer(y_sample, p, tab_sample)
    return (y_prompt, y_sample)
```

```python
import functools
import math

import jax
import jax.numpy as jnp
from jax import lax
from jax.experimental import pallas as pl
from jax.experimental.pallas import tpu as pltpu

D_MODEL = 1024
GRID_W = 64
WIN_R = 8
WIN_C = 16
NA_HEADS = 8
NA_HEAD_DIM = 64
NA_WIDTH = NA_HEADS * NA_HEAD_DIM
FT_GROUPS = 4
FT_GROUP_DIM = 64
FT_WIDTH = FT_GROUPS * FT_GROUP_DIM
CV_WIDTH = 256
CV_KERNEL = 31
NORM_EPS = 1e-6
NEG_INF = -1e30

_Q0, _K0, _V0, _GA0, _UB0, _GB0, _CV0, _CG0, _GC0, _MA0, _MB0, _MC0, _DIN = (
    0, 512, 1024, 1536, 2048, 2304, 2560, 2816, 3072, 3328, 4352, 5376, 6400)

LANES = 128
SUBLANES = 8
MXU_DIM = 256
HEADS_PER_STEP = LANES // NA_HEAD_DIM
VMEM_LIMIT_BYTES = 56 * 1024 * 1024

F32 = jnp.float32
BF16 = jnp.bfloat16


def _params(sem):
    return pltpu.CompilerParams(dimension_semantics=sem, vmem_limit_bytes=VMEM_LIMIT_BYTES)


def _const_spec(shape):
    n = len(shape)
    return pl.BlockSpec(shape, lambda *_: (0,) * n)


def _rms_norm(x, g):
    ms = jnp.mean(x * x, axis=-1, keepdims=True)
    return x * lax.rsqrt(ms + NORM_EPS) * g


def _pitch(block_rows):
    assert block_rows % (2 * SUBLANES) == 0
    return block_rows + SUBLANES


def _blocked_scratch(blocks, block_rows, width):
    return pltpu.VMEM((width // LANES, blocks * _pitch(block_rows), LANES), F32)


def _store_block(ref, block, block_rows, val, first_group=0):
    row = block * _pitch(block_rows)
    if not isinstance(row, int):
        row = pl.multiple_of(row, SUBLANES)
    for g in range(val.shape[-1] // LANES):
        ref[first_group + g, pl.ds(row, block_rows), :] = val[:, g * LANES:(g + 1) * LANES]


def _load_row_of_blocks(ref, row, blocks, block_rows, first_group=0, groups=None):
    groups = ref.shape[0] if groups is None else groups
    return jnp.concatenate([ref[first_group + g, pl.ds(row, blocks, stride=_pitch(block_rows)), :]
                            for g in range(groups)], axis=-1)


def _fft_factors(seq):
    log = int(math.log2(seq))
    assert 1 << log == seq
    l1 = 1 << (log // 2)
    return l1, seq // l1


def _proj_kernel(x_ref, g_ref, w_ref, cs_ref, q_ref, k_ref, v_ref, u_ref, c_ref, us_ref, *, l1):
    x = x_ref[0]
    h = _rms_norm(x, g_ref[...]).astype(BF16)

    def proj(lo, hi):
        return jnp.dot(h, w_ref[:, lo:hi], preferred_element_type=F32)

    q_ref[0] = (proj(0, 512) * (NA_HEAD_DIM ** -0.5)).astype(BF16)
    k_ref[0] = proj(512, 1024).astype(BF16)
    v_ref[0] = proj(1024, 1536).astype(BF16)
    c_val = proj(1792, 2048)
    c_glu = proj(2048, 2304)
    c_ref[0] = (c_val * jax.nn.sigmoid(c_glu)).astype(BF16)
    spec = jnp.dot(proj(1536, 1792).astype(BF16), cs_ref[...], preferred_element_type=F32)
    rows = x.shape[0] // l1
    for c in range(rows):
        _store_block(us_ref, c, l1, spec[c * l1:(c + 1) * l1])
    groups = FT_WIDTH // LANES
    for i in range(l1):
        for part in range(2):
            u_ref[0, i, part] = _load_row_of_blocks(
                us_ref, i, rows, l1, part * groups, groups).astype(BF16)


def _proj_call(x, pre_g, w1, cs, *, tile):
    b, seq, _ = x.shape
    l1, l2 = _fft_factors(seq)
    assert seq % tile == 0 and tile % (16 * l1) == 0
    grid = (b, seq // tile)
    tok = lambda width: pl.BlockSpec((1, tile, width), lambda i, j: (i, j, 0))
    out_shape = (
        jax.ShapeDtypeStruct((b, seq, NA_WIDTH), BF16),
        jax.ShapeDtypeStruct((b, seq, NA_WIDTH), BF16),
        jax.ShapeDtypeStruct((b, seq, NA_WIDTH), BF16),
        jax.ShapeDtypeStruct((b, l1, 2, l2, FT_WIDTH), BF16),
        jax.ShapeDtypeStruct((b, seq, CV_WIDTH), BF16),
    )
    return pl.pallas_call(
        functools.partial(_proj_kernel, l1=l1),
        grid=grid,
        in_specs=[tok(D_MODEL), _const_spec((1, D_MODEL)), _const_spec(w1.shape), _const_spec(cs.shape)],
        out_specs=(tok(NA_WIDTH), tok(NA_WIDTH), tok(NA_WIDTH),
                   pl.BlockSpec((1, l1, 2, tile // l1, FT_WIDTH), lambda i, j: (i, 0, 0, j, 0)),
                   tok(CV_WIDTH)),
        out_shape=out_shape,
        scratch_shapes=[_blocked_scratch(tile // l1, l1, 2 * FT_WIDTH)],
        compiler_params=_params(("parallel", "parallel")),
        name="proj",
    )(x, pre_g, w1, cs)


FNET_UNROLL = 4


def _fnet_kernel(u_ref, f_ref, g_ref, o_ref, ar_ref, ai_ref, w_ref, *, l1, l2, n1, n2, scale):
    width = o_ref.shape[-1]

    def stage1(i, carry):
        a0 = i * n1
        x = u_ref[0, pl.ds(a0, n1)].reshape(n1 * 2 * l2, width)
        res = jnp.dot(f_ref[...], x, preferred_element_type=F32)
        for j in range(n1):
            _store_block(ar_ref, a0 + j, l2, res[2 * j * l2:(2 * j + 1) * l2])
            _store_block(ai_ref, a0 + j, l2, res[(2 * j + 1) * l2:(2 * j + 2) * l2])
        return carry

    lax.fori_loop(0, l1 // n1, stage1, 0, unroll=FNET_UNROLL)

    def stage2(i, carry):
        k0 = i * n2
        parts = []
        for j in range(n2):
            parts.append(_load_row_of_blocks(ar_ref, k0 + j, l1, l2))
            parts.append(_load_row_of_blocks(ai_ref, k0 + j, l1, l2))
        r = jnp.concatenate(parts, axis=0).astype(BF16)
        w = jnp.dot(g_ref[i], r, preferred_element_type=F32) * scale
        for j in range(n2):
            _store_block(w_ref, k0 + j, l1, w[j * l1:(j + 1) * l1])
        return carry

    lax.fori_loop(0, l2 // n2, stage2, 0, unroll=FNET_UNROLL)

    def stage3(k1, carry):
        o_ref[0, pl.ds(pl.multiple_of(k1 * l2, l2), l2), :] = (
            _load_row_of_blocks(w_ref, k1, l2, l1).astype(BF16))
        return carry

    lax.fori_loop(0, l1, stage3, 0, unroll=FNET_UNROLL)


def _fnet_blocking(l1, l2):
    return max(1, MXU_DIM // (2 * l2)), max(1, MXU_DIM // (2 * l1))


def _block_diag(blocks):
    n, r, c = blocks.shape
    eye = jnp.eye(n, dtype=blocks.dtype)
    return (eye[:, None, :, None] * blocks[:, :, None, :]).reshape(n * r, n * c)


def _fnet_tables(seq):
    l1, l2 = _fft_factors(seq)
    n1, n2 = _fnet_blocking(l1, l2)
    ch = jnp.arange(FT_WIDTH, dtype=jnp.int32)
    same = (ch[:, None] // FT_GROUP_DIM) == (ch[None, :] // FT_GROUP_DIM)
    angc = (2.0 * math.pi / FT_GROUP_DIM) * (
        ((ch[:, None] % FT_GROUP_DIM) * (ch[None, :] % FT_GROUP_DIM)) % FT_GROUP_DIM).astype(F32)
    cs = jnp.concatenate([jnp.where(same, jnp.cos(angc), 0.0),
                          jnp.where(same, -jnp.sin(angc), 0.0)], axis=1).astype(BF16)
    i2 = jnp.arange(l2, dtype=jnp.int32)
    ang2 = (2.0 * math.pi / l2) * ((i2[:, None] * i2[None, :]) % l2).astype(F32)
    c2, s2 = jnp.cos(ang2), jnp.sin(ang2)
    f2 = jnp.concatenate([jnp.concatenate([c2, s2], axis=1),
                          jnp.concatenate([-s2, c2], axis=1)], axis=0)
    f = _block_diag(jnp.broadcast_to(f2, (n1,) + f2.shape)).astype(BF16)
    k2 = jnp.arange(l2, dtype=jnp.int32)[:, None, None]
    k1 = jnp.arange(l1, dtype=jnp.int32)[None, :, None]
    a = jnp.arange(l1, dtype=jnp.int32)[None, None, :]
    ang1 = (2.0 * math.pi / seq) * ((a * (l2 * k1 + k2)) % seq).astype(F32)
    g2 = jnp.concatenate([jnp.cos(ang1), jnp.sin(ang1)], axis=2)
    g = jax.vmap(_block_diag)(g2.reshape(l2 // n2, n2, l1, 2 * l1)).astype(BF16)
    return cs, f, g


def _fnet_call(u, f, g, *, width):
    b, l1, _, l2, full = u.shape
    seq = l1 * l2
    n1, n2 = _fnet_blocking(l1, l2)
    scale = 1.0 / math.sqrt(seq * FT_GROUP_DIM)
    return pl.pallas_call(
        functools.partial(_fnet_kernel, l1=l1, l2=l2, n1=n1, n2=n2, scale=scale),
        grid=(b, full // width),
        in_specs=[pl.BlockSpec((1, l1, 2, l2, width), lambda i, j: (i, 0, 0, 0, j)),
                  _const_spec(f.shape), _const_spec(g.shape)],
        out_specs=pl.BlockSpec((1, seq, width), lambda i, j: (i, 0, j)),
        out_shape=jax.ShapeDtypeStruct((b, seq, full), BF16),
        scratch_shapes=[_blocked_scratch(l1, l2, width), _blocked_scratch(l1, l2, width),
                        _blocked_scratch(l2, l1, width)],
        compiler_params=_params(("parallel", "parallel")),
        name="fnet",
    )(u, f, g)


CONV_ROWS = 32
CONV_HALO = 16
assert CV_KERNEL // 2 < CONV_HALO


CONV_SPAN = CONV_ROWS + 2 * CONV_HALO


def _conv_rows(win, w_ref, bias):
    reps = CONV_ROWS // SUBLANES
    out = []
    for lo in range(0, win.shape[-1], LANES):
        lanes = slice(lo, lo + LANES)
        acc = jnp.zeros((CONV_ROWS, LANES), F32) + bias[:, lanes]
        for r in range(SUBLANES):
            sh = win[:, lanes] if r == 0 else pltpu.roll(win[:, lanes], CONV_SPAN - r, axis=0)
            for mlt in range(CONV_SPAN // SUBLANES):
                j = SUBLANES * mlt + r - (CONV_HALO - CV_KERNEL // 2)
                if 0 <= j < CV_KERNEL and SUBLANES * mlt + r + CONV_ROWS <= CONV_SPAN:
                    wj = jnp.concatenate([w_ref[j, :, lanes]] * reps, axis=0)
                    acc = acc + wj * sh[SUBLANES * mlt:SUBLANES * mlt + CONV_ROWS]
        out.append(acc)
    return jnp.concatenate(out, axis=-1)


NATT_BODY_ROWS = 4
NATT_STEP_ROWS = 256


def _natt_kernel(q_ref, k_ref, v_ref, t_ref, o_ref, s0_ref, s1_ref, p0_ref, p1_ref, d0_ref, d1_ref,
                 *, rows):
    seqs = q_ref.shape[0]
    n_rows = seqs * rows
    lane = lax.broadcasted_iota(jnp.int32, (GRID_W, LANES), 1)
    first = lane < NA_HEAD_DIM
    win = WIN_R * GRID_W
    s_banks, p_banks, d_banks = (s0_ref, s1_ref), (p0_ref, p1_ref), (d0_ref, d1_ref)
    row_bits = rows.bit_length() - 1
    assert 1 << row_bits == rows

    def locate(i):
        if isinstance(i, int):
            b, r = divmod(i, rows)
            r_start = min(max(r - WIN_R // 2, 0), rows - WIN_R)
            return b, r * GRID_W, r_start * GRID_W, r - r_start
        b = lax.shift_right_logical(i, row_bits)
        r = lax.bitwise_and(i, rows - 1)
        r_start = jnp.clip(r - WIN_R // 2, 0, rows - WIN_R)
        return (b, pl.multiple_of(r * GRID_W, GRID_W), pl.multiple_of(r_start * GRID_W, GRID_W),
                r - r_start)

    def scores(i, s_ref, slot):
        b, qrow, krow, edge = locate(i)
        q = q_ref[b, pl.ds(qrow, GRID_W), :].astype(F32)
        q2 = jnp.concatenate([jnp.where(first, q, 0.0), jnp.where(first, 0.0, q)], axis=0).astype(BF16)
        kw = k_ref[b, pl.ds(krow, win), :]
        s = lax.dot_general(q2, kw, (((1,), (1,)), ((), ())), preferred_element_type=F32)
        s_ref[slot] = s + t_ref[0, edge]

    def softmax(s_ref, p_ref, d_ref, slot):
        s = s_ref[slot]
        p = jnp.exp(s - jnp.max(s, axis=-1, keepdims=True))
        d_ref[slot] = jnp.broadcast_to(jnp.sum(p, axis=-1, keepdims=True), d_ref.shape[1:])
        p_ref[slot] = p.astype(BF16)

    def output(i, p_ref, d_ref, slot):
        b, qrow, krow, _ = locate(i)
        vw = v_ref[b, pl.ds(krow, win), :]
        o = jnp.dot(p_ref[slot], vw, preferred_element_type=F32) / d_ref[slot]
        o_ref[b, pl.ds(qrow, GRID_W), :] = jnp.where(first, o[:GRID_W], o[GRID_W:]).astype(BF16)

    def body(m, bank, *, do_scores=True, do_softmax=True, do_output=True):
        for slot in range(NATT_BODY_ROWS):
            if do_output:
                output((m - 2) * NATT_BODY_ROWS + slot, p_banks[bank], d_banks[bank], slot)
        for slot in range(NATT_BODY_ROWS):
            if do_softmax:
                softmax(s_banks[1 - bank], p_banks[1 - bank], d_banks[1 - bank], slot)
        for slot in range(NATT_BODY_ROWS):
            if do_scores:
                scores(m * NATT_BODY_ROWS + slot, s_banks[bank], slot)

    n_bodies = n_rows // NATT_BODY_ROWS
    assert n_bodies % 2 == 0 and n_bodies >= 2
    body(0, 0, do_softmax=False, do_output=False)
    body(1, 1, do_output=False)

    def steady(j, carry):
        body(2 * j, 0)
        body(2 * j + 1, 1)
        return carry

    lax.fori_loop(1, n_bodies // 2, steady, 0)
    body(n_bodies, 0, do_scores=False)
    body(n_bodies + 1, 1, do_scores=False, do_softmax=False)


def _natt_table(rpb):
    col = jnp.arange(GRID_W)
    c_start = jnp.clip(col - WIN_C // 2, 0, GRID_W - WIN_C)
    valid = (col[None, :] >= c_start[:, None]) & (col[None, :] < c_start[:, None] + WIN_C)
    dc = jnp.clip(col[None, :] - col[:, None] + WIN_C - 1, 0, 2 * WIN_C - 2)
    by_dr = jnp.where(valid[None, None], rpb[:, :, dc], NEG_INF)
    t = jnp.stack([by_dr[:, WIN_R - 1 - e:2 * WIN_R - 1 - e] for e in range(WIN_R)], axis=1)
    t = t.transpose(0, 1, 3, 2, 4).reshape(NA_HEADS // HEADS_PER_STEP, HEADS_PER_STEP, WIN_R,
                                           GRID_W, WIN_R * GRID_W)
    return t.transpose(0, 2, 1, 3, 4).reshape(NA_HEADS // HEADS_PER_STEP, WIN_R,
                                              HEADS_PER_STEP * GRID_W, WIN_R * GRID_W).astype(F32)


def _natt_call(q, k, v, table):
    b, seq, _ = q.shape
    rows = seq // GRID_W
    assert rows >= WIN_R
    seqs = max(1, min(b, NATT_STEP_ROWS // rows))
    assert b % seqs == 0
    blk = pl.BlockSpec((seqs, seq, LANES), lambda i, j: (i, 0, j))
    stacked = HEADS_PER_STEP * GRID_W
    scratch = ([pltpu.VMEM((NATT_BODY_ROWS, stacked, WIN_R * GRID_W), F32)] * 2
               + [pltpu.VMEM((NATT_BODY_ROWS, stacked, WIN_R * GRID_W), BF16)] * 2
               + [pltpu.VMEM((NATT_BODY_ROWS, stacked, LANES), F32)] * 2)
    return pl.pallas_call(
        functools.partial(_natt_kernel, rows=rows),
        grid=(b // seqs, NA_HEADS // HEADS_PER_STEP),
        in_specs=[blk, blk, blk,
                  pl.BlockSpec((1,) + table.shape[1:], lambda i, j: (j, 0, 0, 0))],
        out_specs=blk,
        out_shape=jax.ShapeDtypeStruct((b, seq, NA_WIDTH), BF16),
        scratch_shapes=scratch,
        compiler_params=_params(("parallel", "parallel")),
        name="natt",
    )(q, k, v, table)


def _out_kernel(x_ref, ya_ref, yb_ref, c_ref, cprev_ref, cnext_ref, pre_ref, post_ref, cg_ref, cb_ref,
                dw_ref, db_ref, w2_ref, wa_ref, wb_ref, wc_ref, wo_ref, o_ref, pad_ref, conv_ref, *, sub):
    tile = x_ref.shape[1]
    j = pl.program_id(1)
    keep_prev = (j > 0).astype(F32)
    keep_next = (j < pl.num_programs(1) - 1).astype(F32)
    pad_ref[0:CONV_HALO] = cprev_ref[0].astype(F32) * keep_prev
    pad_ref[CONV_HALO:CONV_HALO + tile] = c_ref[0].astype(F32)
    pad_ref[CONV_HALO + tile:2 * CONV_HALO + tile] = cnext_ref[0].astype(F32) * keep_next

    for r0 in range(0, tile, sub):
        rows = pl.ds(r0, sub)
        x = x_ref[0, rows]
        h = _rms_norm(x, pre_ref[...]).astype(BF16)

        def proj(lo, hi):
            return jnp.dot(h, w2_ref[:, lo:hi], preferred_element_type=F32)

        def branch(y, gate_lo, gate_hi, w_ref, merge_lo):
            gated = (y * jax.nn.silu(proj(gate_lo, gate_hi))).astype(BF16)
            p = jnp.dot(gated, w_ref[...], preferred_element_type=F32)
            return jax.nn.sigmoid(proj(merge_lo, merge_lo + D_MODEL)) * p

        for start in range(r0, r0 + sub, CONV_ROWS):
            conv_ref[start:start + CONV_ROWS] = _conv_rows(
                pad_ref[start:start + CONV_SPAN], dw_ref, db_ref[...])
        conv = conv_ref[rows]
        mu = jnp.mean(conv, axis=-1, keepdims=True)
        cen = conv - mu
        var = jnp.mean(cen * cen, axis=-1, keepdims=True)
        y_c = jax.nn.silu(cen * lax.rsqrt(var + NORM_EPS) * cg_ref[...] + cb_ref[...])

        merged = branch(ya_ref[0, rows].astype(F32), 0, 512, wa_ref, 1024)
        merged = merged + branch(yb_ref[0, rows].astype(F32), 512, 768, wb_ref, 2048)
        merged = merged + branch(y_c, 768, 1024, wc_ref, 3072)
        o = jnp.dot(merged.astype(BF16), wo_ref[...], preferred_element_type=F32)
        o_ref[0, rows] = x + _rms_norm(o, post_ref[...])


def _out_call(x, ya, yb, c, pre_g, post_g, cn_g, cn_b, dw_w, dw_b, w2, wa, wb, wc, wo, *, tile, sub):
    b, seq, _ = x.shape
    assert seq % tile == 0 and tile % sub == 0 and sub % CONV_ROWS == 0 and tile % CONV_HALO == 0
    tok = lambda width: pl.BlockSpec((1, tile, width), lambda i, j: (i, j, 0))
    halo_blocks = tile // CONV_HALO
    last_halo = seq // CONV_HALO - 1
    prev = pl.BlockSpec((1, CONV_HALO, CV_WIDTH),
                        lambda i, j: (i, jnp.maximum(j * halo_blocks - 1, 0), 0))
    nxt = pl.BlockSpec((1, CONV_HALO, CV_WIDTH),
                       lambda i, j: (i, jnp.minimum((j + 1) * halo_blocks, last_halo), 0))
    consts = (pre_g, post_g, cn_g, cn_b, dw_w, dw_b, w2, wa, wb, wc, wo)
    return pl.pallas_call(
        functools.partial(_out_kernel, sub=sub),
        grid=(b, seq // tile),
        in_specs=[tok(D_MODEL), tok(NA_WIDTH), tok(FT_WIDTH), tok(CV_WIDTH), prev, nxt]
                 + [_const_spec(const.shape) for const in consts],
        out_specs=tok(D_MODEL),
        out_shape=jax.ShapeDtypeStruct(x.shape, x.dtype),
        scratch_shapes=[pltpu.VMEM((tile + 2 * CONV_HALO, CV_WIDTH), F32),
                        pltpu.VMEM((tile, CV_WIDTH), F32)],
        compiler_params=_params(("parallel", "parallel")),
        name="out",
    )(x, ya, yb, c, c, c, *consts)


PROJ_TILE = 1024
OUT_TILE = 1024
OUT_SUB = 512
FNET_SCRATCH_BYTES = 16 * 1024 * 1024


def _fnet_width(seq):
    width = FT_WIDTH
    while 3 * seq * width * 4 > FNET_SCRATCH_BYTES and width > LANES:
        width //= 2
    return width


def _layer_weights(pre_g, post_g, w_in, rpb, dw_w, dw_b, cn_g, cn_b, w_a, w_b, w_c, w_o):
    cols = lambda *ranges: jnp.concatenate([w_in[:, lo:hi] for lo, hi in ranges], axis=1).astype(BF16)
    row = lambda vec: vec.reshape(1, -1).astype(F32)
    return dict(
        w1=cols((_Q0, _GA0), (_UB0, _GB0), (_CV0, _GC0)),
        w2=cols((_GA0, _UB0), (_GB0, _CV0), (_GC0, _DIN)),
        table=_natt_table(rpb),
        pre_g=row(pre_g), post_g=row(post_g), cn_g=row(cn_g), cn_b=row(cn_b),
        dw_w=jnp.broadcast_to(dw_w.astype(F32)[:, None, :], (CV_KERNEL, SUBLANES, CV_WIDTH)),
        dw_b=row(dw_b),
        wa=w_a.astype(BF16), wb=w_b.astype(BF16), wc=w_c.astype(BF16), wo=w_o.astype(BF16),
    )


def _encoder_layer(x, p, tables, *, proj_tile=PROJ_TILE, out_tile=OUT_TILE, out_sub=OUT_SUB):
    cs, f, g = tables
    q, k, v, u, c = _proj_call(x, p["pre_g"], p["w1"], cs, tile=proj_tile)
    yb = _fnet_call(u, f, g, width=_fnet_width(x.shape[1]))
    ya = _natt_call(q, k, v, p["table"])
    return _out_call(x, ya, yb, c, p["pre_g"], p["post_g"], p["cn_g"], p["cn_b"], p["dw_w"], p["dw_b"],
                     p["w2"], p["wa"], p["wb"], p["wc"], p["wo"], tile=out_tile, sub=out_sub)


def kernel(x_prompt, x_sample, pre_norm_g, post_norm_g, w_in, rel_pos_bias, c_dw_w, c_dw_b,
           c_norm_g, c_norm_b, w_a_out, w_b_out, w_c_out, w_o):
    depth = w_in.shape[0]
    tab_prompt = _fnet_tables(x_prompt.shape[1])
    tab_sample = _fnet_tables(x_sample.shape[1])
    y_prompt, y_sample = x_prompt, x_sample
    for i in range(depth):
        p = _layer_weights(pre_norm_g[i], post_norm_g[i], w_in[i], rel_pos_bias[i], c_dw_w[i],
                           c_dw_b[i], c_norm_g[i], c_norm_b[i], w_a_out[i], w_b_out[i],
                           w_c_out[i], w_o[i])
        y_prompt = _encoder_layer(y_prompt, p, tab_prompt)
        y_sample = _encoder_layer(y_sample, p, tab_sample)
    return (y_prompt, y_sample)
```
